```python
import jax, jax.numpy as jnp
from jax import lax
import numpy as np

D_MODEL = 1024
BATCH = 16
SEQ = 256
DEPTH = 2
DEC_BATCH = 4
DEC_SEQ = 1024
PAST_LEN = 256

GRID_W = 64
GLA_HEADS = 4
GLA_DK = 128
GLA_DV = 256
GLA_LOWRANK = 16
GLA_TAU = 16.0
GLA_CHUNK = 64
FNET_GROUPS = 4
FNET_DG = 128
PEER_HEADS = 8
PEER_NKEYS = 128
PEER_DQ = 256
PEER_TOPK = 16
PEER_EXPERTS = PEER_NKEYS * PEER_NKEYS
PEER_BLOCK = 128
ROPE_THETA = 10000.0
EPS = 1e-6
QK_W = GLA_HEADS * GLA_DK
V_W = GLA_HEADS * GLA_DV
F_W = FNET_GROUPS * FNET_DG
IN_W = 2 * QK_W + 2 * V_W + 2 * GLA_LOWRANK + F_W + 2 * D_MODEL

kernel_name = "hybrid_gla_fnet_peer_diffusion_step"


def _rmsnorm(x, w):
    xf = x.astype(jnp.float32)
    y = xf * lax.rsqrt(jnp.mean(xf * xf, axis=-1, keepdims=True) + EPS)
    return (y * w).astype(x.dtype)


def _modulation(cond, w_mod, b_mod):
    m = jax.nn.silu(cond) @ w_mod + b_mod
    return jnp.split(m[:, None, :], 6, axis=-1)


def _heads(t, d):
    B, T, _ = t.shape
    return t.reshape(B, T, -1, d).transpose(0, 2, 1, 3)


def _rope_part(x, pos):
    half = x.shape[-1] // 2
    freqs = ROPE_THETA ** (-jnp.arange(half, dtype=jnp.float32) / half)
    ang = pos.astype(jnp.float32)[:, None] * freqs[None, :]
    cos = jnp.cos(ang).astype(x.dtype)
    sin = jnp.sin(ang).astype(x.dtype)
    x1, x2 = x[..., :half], x[..., half:]
    return jnp.concatenate([x1 * cos - x2 * sin, x1 * sin + x2 * cos], axis=-1)


def _rope_2d(x, row, col):
    d = x.shape[-1] // 2
    return jnp.concatenate([_rope_part(x[..., :d], row), _rope_part(x[..., d:], col)], axis=-1)


def _gla_chunked(q, k, v, g, s0):
    B, H, T, dk = q.shape
    dv = v.shape[-1]
    C = GLA_CHUNK
    N = T // C
    f32 = jnp.float32
    q = q.astype(f32).reshape(B, H, N, C, dk)
    k = k.astype(f32).reshape(B, H, N, C, dk)
    v = v.astype(f32).reshape(B, H, N, C, dv)
    g = g.astype(f32).reshape(B, H, N, C, dk)
    b = jnp.cumsum(g, axis=3)
    b_last = b[:, :, :, -1:, :]
    q_dec = q * jnp.exp(b)
    att = jnp.einsum('bhncd,bhnsd->bhncs', q_dec, k * jnp.exp(-b))
    att = jnp.where(jnp.tril(jnp.ones((C, C), bool)), att, 0.0)
    o_intra = jnp.einsum('bhncs,bhnse->bhnce', att, v)
    d_state = jnp.einsum('bhncd,bhnce->nbhde', k * jnp.exp(b_last - b), v)
    decay = jnp.exp(b_last[:, :, :, 0, :]).transpose(2, 0, 1, 3)

    def step(s, inp):
        dec, ds = inp
        return s * dec[..., None] + ds, s

    s_fin, s_prev = lax.scan(step, s0.astype(f32), (decay, d_state))
    o_inter = jnp.einsum('bhncd,nbhde->bhnce', q_dec, s_prev)
    return (o_intra + o_inter).reshape(B, H, T, dv), s_fin


def _bidir_gla(q, k, v, g_f, g_b, s_f0, s_b0):
    o_f, s_f = _gla_chunked(q, k, v, g_f, s_f0)
    flip = lambda t: t[:, :, ::-1]
    o_b, s_b = _gla_chunked(flip(q), flip(k), flip(v), flip(g_b), s_b0)
    return o_f + flip(o_b), s_f, s_b


def _token_mixer(h, rope_pos, s_f0, s_b0, w_in, w_af, b_af, w_ab, b_ab,
                 gla_norm, w_gla_out, w_four_out, w_out):
    B, T, _ = h.shape
    sizes = (QK_W, QK_W, V_W, V_W, GLA_LOWRANK, GLA_LOWRANK, F_W, D_MODEL, D_MODEL)
    cuts = np.cumsum(sizes)[:-1].tolist()
    q, k, v, r, lr_f, lr_b, u, z_gla, z_four = jnp.split(h @ w_in, cuts, axis=-1)
    q = _heads(q, GLA_DK) * (GLA_DK ** -0.5)
    k = _heads(k, GLA_DK)
    v = _heads(v, GLA_DV)
    if rope_pos is not None:
        row, col = rope_pos
        q = _rope_2d(q, row, col)
        k = _rope_2d(k, row, col)
    g_f = _heads(jax.nn.log_sigmoid((lr_f @ w_af + b_af).astype(jnp.float32)) / GLA_TAU, GLA_DK)
    g_b = _heads(jax.nn.log_sigmoid((lr_b @ w_ab + b_ab).astype(jnp.float32)) / GLA_TAU, GLA_DK)
    o, s_f, s_b = _bidir_gla(q, k, v, g_f, g_b, s_f0, s_b0)
    o = o * lax.rsqrt(jnp.mean(o * o, axis=-1, keepdims=True) + EPS)
    o = o.transpose(0, 2, 1, 3).reshape(B, T, V_W).astype(h.dtype) * gla_norm
    y_gla = (o * jax.nn.silu(r)) @ w_gla_out
    uf = u.reshape(B, T, FNET_GROUPS, FNET_DG).astype(jnp.float32)
    four = jnp.real(jnp.fft.fft2(uf, axes=(1, 3), norm='ortho')).reshape(B, T, F_W).astype(h.dtype)
    y_four = four @ w_four_out
    merged = jax.nn.sigmoid(z_gla) * y_gla + jax.nn.sigmoid(z_four) * y_four
    return merged @ w_out, s_f.astype(h.dtype), s_b.astype(h.dtype)


def _peer(h, w_query, sub_keys, expert_u, expert_v):
    B, T, D = h.shape
    n = B * T
    x = h.reshape(n, D)
    qry = (x @ w_query).reshape(n, PEER_HEADS, 2, PEER_DQ // 2)
    s = jnp.einsum('nhpd,hpkd->nhpk', qry, sub_keys).astype(jnp.float32)
    s1, i1 = lax.top_k(s[:, :, 0], PEER_TOPK)
    s2, i2 = lax.top_k(s[:, :, 1], PEER_TOPK)
    cand = (s1[..., :, None] + s2[..., None, :]).reshape(n, PEER_HEADS, PEER_TOPK * PEER_TOPK)
    cidx = (i1[..., :, None] * PEER_NKEYS + i2[..., None, :]).reshape(n, PEER_HEADS, PEER_TOPK * PEER_TOPK)
    top_s, sel = lax.top_k(cand, PEER_TOPK)
    idx = jnp.take_along_axis(cidx, sel, axis=-1)
    w = jax.nn.softmax(top_s, axis=-1)
    nblk = n // PEER_BLOCK

    def block(args):
        xb, ib, wb = args
        a = jax.nn.gelu(jnp.einsum('nd,nhkd->nhk', xb, expert_u[ib]).astype(jnp.float32))
        coef = (wb * a).astype(xb.dtype)
        return jnp.einsum('nhk,nhkd->nd', coef, expert_v[ib])

    out = lax.map(block, (x.reshape(nblk, PEER_BLOCK, D),
                          idx.reshape(nblk, PEER_BLOCK, PEER_HEADS, PEER_TOPK),
                          w.reshape(nblk, PEER_BLOCK, PEER_HEADS, PEER_TOPK)))
    return out.reshape(B, T, D)


def _trunk(x, cond, rope_pos, init_state, w_mod, b_mod, norm1, w_in, w_af, b_af, w_ab, b_ab,
           gla_norm, w_gla_out, w_four_out, w_out, norm2, w_query, sub_keys, expert_u,
           expert_v, final_norm):
    B = x.shape[0]
    finals = []
    for l in range(DEPTH):
        if init_state is None:
            s_f0 = jnp.zeros((B, GLA_HEADS, GLA_DK, GLA_DV), x.dtype)
            s_b0 = s_f0
        else:
            s_f0 = init_state[:, l, 0]
            s_b0 = init_state[:, l, 1]
        sh1, sc1, g1, sh2, sc2, g2 = _modulation(cond, w_mod[l], b_mod[l])
        h = _rmsnorm(x, norm1[l]) * (1 + sc1) + sh1
        mix, s_f, s_b = _token_mixer(h, rope_pos, s_f0, s_b0, w_in[l], w_af[l], b_af[l], w_ab[l],
                                     b_ab[l], gla_norm[l], w_gla_out[l], w_four_out[l], w_out[l])
        x = x + g1 * mix
        h = _rmsnorm(x, norm2[l]) * (1 + sc2) + sh2
        x = x + g2 * _peer(h, w_query[l], sub_keys[l], expert_u[l], expert_v[l])
        if init_state is None:
            finals.append(jnp.stack([s_f, s_b], axis=1))
    y = _rmsnorm(x, final_norm)
    if init_state is None:
        return y, jnp.stack(finals, axis=1)
    return y, None


def setup_inputs(seed: int = 0) -> dict:
    key = jax.random.key(seed)
    ks = jax.random.split(key, 24)
    f32 = jnp.float32
    nrm = lambda k, shape, s: jax.random.normal(k, shape, f32) * s
    D = D_MODEL
    return {
        "x_prompt": nrm(ks[0], (BATCH, SEQ, D), 1.0),
        "x_sample": nrm(ks[1], (DEC_BATCH, DEC_SEQ, D), 1.0),
        "state_gla": nrm(ks[2], (DEC_BATCH, DEPTH, 2, GLA_HEADS, GLA_DK, GLA_DV), 2.0),
        "c": nrm(ks[3], (DEC_BATCH, D), 1.0),
        "c_ctx": nrm(ks[4], (D,), 1.0),
        "w_mod": nrm(ks[5], (DEPTH, D, 6 * D), 0.5 * D ** -0.5),
        "b_mod": nrm(ks[6], (DEPTH, 6 * D), 0.02),
        "norm1": 1.0 + nrm(ks[7], (DEPTH, D), 0.02),
        "w_in": nrm(ks[8], (DEPTH, D, IN_W), D ** -0.5),
        "w_af": nrm(ks[9], (DEPTH, GLA_LOWRANK, QK_W), GLA_LOWRANK ** -0.5),
        "b_af": nrm(ks[10], (DEPTH, QK_W), 0.1),
        "w_ab": nrm(ks[11], (DEPTH, GLA_LOWRANK, QK_W), GLA_LOWRANK ** -0.5),
        "b_ab": nrm(ks[12], (DEPTH, QK_W), 0.1),
        "gla_norm": 1.0 + nrm(ks[13], (DEPTH, V_W), 0.02),
        "w_gla_out": nrm(ks[14], (DEPTH, V_W, D), V_W ** -0.5),
        "w_four_out": nrm(ks[15], (DEPTH, F_W, D), F_W ** -0.5),
        "w_out": nrm(ks[16], (DEPTH, D, D), D ** -0.5),
        "norm2": 1.0 + nrm(ks[17], (DEPTH, D), 0.02),
        "w_query": nrm(ks[18], (DEPTH, D, PEER_HEADS * PEER_DQ), D ** -0.5),
        "sub_keys": nrm(ks[19], (DEPTH, PEER_HEADS, 2, PEER_NKEYS, PEER_DQ // 2), (PEER_DQ // 2) ** -0.5),
        "expert_u": nrm(ks[20], (DEPTH, PEER_EXPERTS, D), D ** -0.5),
        "expert_v": nrm(ks[21], (DEPTH, PEER_EXPERTS, D), 1.0),
        "final_norm": 1.0 + nrm(ks[22], (D,), 0.02),
    }


def reference(x_prompt, x_sample, state_gla, c, c_ctx, w_mod, b_mod, norm1, w_in, w_af, b_af,
              w_ab, b_ab, gla_norm, w_gla_out, w_four_out, w_out, norm2, w_query, sub_keys,
              expert_u, expert_v, final_norm):
    weights = (w_mod, b_mod, norm1, w_in, w_af, b_af, w_ab, b_ab, gla_norm, w_gla_out,
               w_four_out, w_out, norm2, w_query, sub_keys, expert_u, expert_v, final_norm)
    y_prompt, new_state_gla = _trunk(x_prompt, c_ctx[None, :], None, None, *weights)
    T = x_sample.shape[1]
    rows = T // GRID_W
    row = jnp.repeat(jnp.arange(rows), GRID_W)
    col = jnp.tile(jnp.arange(GRID_W), rows)
    y_sample, _ = _trunk(x_sample, c, (row, col), state_gla, *weights)
    return (y_prompt, y_sample, new_state_gla)
```

```python
import functools

import numpy as np
import jax
import jax.numpy as jnp
from jax import lax
from jax.experimental import pallas as pl
from jax.experimental.pallas import tpu as pltpu

f32 = jnp.float32
bf16 = jnp.bfloat16

D = 1024
DEPTH = 2
BATCH, SEQ = 16, 256
DEC_BATCH, DEC_SEQ = 4, 1024
N_PROMPT = BATCH * SEQ
NTOK = N_PROMPT + DEC_BATCH * DEC_SEQ
GRID_W = 64
H, DK, DV = 4, 128, 256
LOWRANK = 16
GLA_TAU = 16.0
CHUNK = 64
FG, FDG = 4, 128
F_W = FG * FDG
PH, NKEYS, DQ, TOPK = 8, 128, 256, 16
NEXP = NKEYS * NKEYS
ROPE_THETA = 10000.0
EPS = 1e-6
QK_W, V_W = H * DK, H * DV

PW = 5760
COL_LR = 5632
SB = 1024
NSB = NTOK // SB
N_PROMPT_SB = N_PROMPT // SB
VMEM_LIMIT = 56 * 1024 * 1024


def _dot(a, b):
    return jnp.dot(a, b, preferred_element_type=f32)


def _dot_nt(a, b):
    return lax.dot_general(a, b, (((1,), (1,)), ((), ())), preferred_element_type=f32)


def _dot_tn(a, b):
    return lax.dot_general(a, b, (((0,), (0,)), ((), ())), preferred_element_type=f32)


def _params(n_axes):
    return pltpu.CompilerParams(dimension_semantics=("arbitrary",) * n_axes,
                                vmem_limit_bytes=VMEM_LIMIT)


def _cond_of_block(i, rows):
    first = N_PROMPT // rows
    return jnp.where(i < first, 0, 1 + (i - first) // (DEC_SEQ // rows))


def _mod_kernel(c_ref, w_ref, b_ref, o_ref):
    c = c_ref[...]
    s = c * jax.nn.sigmoid(c)
    o_ref[0] = _dot(s.astype(bf16), w_ref[0].astype(bf16)) + b_ref[0]


def _modulation(cond8, w_mod, b_mod):
    nt = 6
    return pl.pallas_call(
        _mod_kernel,
        grid=(DEPTH, nt),
        in_specs=[pl.BlockSpec((8, D), lambda l, n: (0, 0)),
                  pl.BlockSpec((1, D, D), lambda l, n: (l, 0, n)),
                  pl.BlockSpec((1, 1, D), lambda l, n: (l, 0, n))],
        out_specs=pl.BlockSpec((1, 8, D), lambda l, n: (l, 0, n)),
        out_shape=jax.ShapeDtypeStruct((DEPTH, 8, 6 * D), f32),
        compiler_params=_params(2),
        name="modulation",
    )(cond8, w_mod, b_mod.reshape(DEPTH, 1, 6 * D))


def _inproj_kernel(x_ref, mod_ref, nw_ref, w_ref, o_ref):
    x = x_ref[...]
    y = x * lax.rsqrt(jnp.mean(x * x, axis=-1, keepdims=True) + EPS) * nw_ref[...]
    sh = mod_ref[0, :, 0:D]
    sc = mod_ref[0, :, D:2 * D]
    h = y * (1.0 + sc) + sh
    o_ref[...] = _dot(h.astype(bf16), w_ref[...])


def _inproj(x, mods, norm_w, w_perm):
    tm, tn = 512, 1920
    return pl.pallas_call(
        _inproj_kernel,
        grid=(PW // tn, NTOK // tm),
        in_specs=[pl.BlockSpec((tm, D), lambda n, m: (m, 0)),
                  pl.BlockSpec((1, 1, 2 * D), lambda n, m: (_cond_of_block(m, tm), 0, 0)),
                  pl.BlockSpec((1, D), lambda n, m: (0, 0)),
                  pl.BlockSpec((D, tn), lambda n, m: (0, n))],
        out_specs=pl.BlockSpec((tm, tn), lambda n, m: (m, n)),
        out_shape=jax.ShapeDtypeStruct((NTOK, PW), f32),
        compiler_params=_params(2),
        name="inproj",
    )(x, mods, norm_w, w_perm)


def _log_sigmoid(x):
    return jnp.minimum(x, 0.0) - jnp.log1p(jnp.exp(-jnp.abs(x)))


def _gla_kernel(q_ref, k_ref, v_ref, lr_ref, waf_ref, wab_ref, baf_ref, bab_ref, cos_ref, sin_ref,
                s0_ref, o_ref, st_ref, q_s, k_s, gf_s, gb_s):
    j = pl.program_id(0)
    is_sample = j >= N_PROMPT_SB

    lr = lr_ref[...].astype(bf16)
    gf_s[...] = _log_sigmoid(_dot(lr, waf_ref[...]) + baf_ref[...]) * (1.0 / GLA_TAU)
    gb_s[...] = _log_sigmoid(_dot(lr, wab_ref[...]) + bab_ref[...]) * (1.0 / GLA_TAU)

    cos = cos_ref[0]
    sin = sin_ref[0]
    lane = lax.broadcasted_iota(jnp.int32, (1, DK), 1)
    first_half = (lane % 64) < 32

    def rope(x):
        partner = jnp.where(first_half, pltpu.roll(x, DK - 32, 1), pltpu.roll(x, 32, 1))
        return x * cos + partner * sin

    q_s[...] = rope(q_ref[...] * (DK ** -0.5))
    k_s[...] = rope(k_ref[...])

    r_i = lax.broadcasted_iota(jnp.int32, (CHUNK, CHUNK), 0)
    c_i = lax.broadcasted_iota(jnp.int32, (CHUNK, CHUNK), 1)
    n_chunks = SB // CHUNK
    per_seq = SEQ // CHUNK

    def run(direction):
        fwd = direction == 0
        mask = (r_i >= c_i) if fwd else (r_i <= c_i)
        tri = mask.astype(bf16)
        g_s = gf_s if fwd else gb_s
        last = CHUNK - 1 if fwd else 0
        s0 = jnp.where(is_sample, s0_ref[0, direction, 0], 0.0)
        st = s0.T
        order = range(n_chunks) if fwd else range(n_chunks - 1, -1, -1)
        for n in order:
            sl = slice(n * CHUNK, (n + 1) * CHUNK)
            starts_seq = (n % per_seq == 0) if fwd else (n % per_seq == per_seq - 1)
            ends_seq = (n % per_seq == per_seq - 1) if fwd else (n % per_seq == 0)
            if starts_seq and n != order[0]:
                st = jnp.where(is_sample, st, 0.0)
            g = g_s[sl, :]
            g_hi = g.astype(bf16)
            g_lo = (g - g_hi.astype(f32)).astype(bf16)
            b = _dot(tri, g_hi) + _dot(tri, g_lo)
            b_last = b[last:last + 1, :]
            q_c = q_s[sl, :]
            k_c = k_s[sl, :]
            q_dec = (q_c * jnp.exp(b)).astype(bf16)
            k_inv = (k_c * jnp.exp(-b)).astype(bf16)
            k_dec = (k_c * jnp.exp(b_last - b)).astype(bf16)
            v_c = v_ref[sl, :].astype(bf16)
            att = jnp.where(mask, _dot_nt(q_dec, k_inv), 0.0).astype(bf16)
            o_c = _dot(att, v_c) + _dot_nt(q_dec, st.astype(bf16))
            if fwd:
                o_ref[sl, :] = o_c
            else:
                o_ref[sl, :] += o_c
            st = st * jnp.exp(b_last) + _dot_tn(v_c, k_dec)
            if ends_seq:
                st_ref[0, n // per_seq, direction, 0] = st.T

    run(0)
    run(1)
    o = o_ref[...]
    o_ref[...] = o * lax.rsqrt(jnp.mean(o * o, axis=-1, keepdims=True) + EPS)


def _gla(p, waf_pad, wab_pad, b_af, b_ab, cos_t, sin_t, s0):
    col = lambda j, h: (j, h)
    return pl.pallas_call(
        _gla_kernel,
        grid=(NSB, H),
        in_specs=[pl.BlockSpec((SB, DK), lambda j, h: (j, h)),
                  pl.BlockSpec((SB, DK), lambda j, h: (j, QK_W // DK + h)),
                  pl.BlockSpec((SB, DV), lambda j, h: (j, 2 * QK_W // DV + h)),
                  pl.BlockSpec((SB, 128), lambda j, h: (j, COL_LR // 128)),
                  pl.BlockSpec((128, DK), lambda j, h: (0, h)),
                  pl.BlockSpec((128, DK), lambda j, h: (0, h)),
                  pl.BlockSpec((1, DK), lambda j, h: (0, h)),
                  pl.BlockSpec((1, DK), lambda j, h: (0, h)),
                  pl.BlockSpec((1, SB, DK), lambda j, h: (jnp.where(j >= N_PROMPT_SB, 1, 0), 0, 0)),
                  pl.BlockSpec((1, SB, DK), lambda j, h: (jnp.where(j >= N_PROMPT_SB, 1, 0), 0, 0)),
                  pl.BlockSpec((1, 2, 1, DK, DV),
                               lambda j, h: (jnp.maximum(j - N_PROMPT_SB, 0), 0, h, 0, 0))],
        out_specs=[pl.BlockSpec((SB, DV), col),
                   pl.BlockSpec((1, SB // SEQ, 2, 1, DK, DV), lambda j, h: (j, 0, 0, h, 0, 0))],
        out_shape=[jax.ShapeDtypeStruct((NTOK, V_W), f32),
                   jax.ShapeDtypeStruct((NSB, SB // SEQ, 2, H, DK, DV), f32)],
        scratch_shapes=[pltpu.VMEM((SB, DK), f32)] * 4,
        compiler_params=_params(2),
        name="gla",
    )(p, p, p, p, waf_pad, wab_pad, b_af, b_ab, cos_t, sin_t, s0)


def _fnet_kernel(u_ref, cs_ref, c256_ref, s256_ref, c1k_ref, s1k_ref, o_ref, uc_s, us_s):
    j = pl.program_id(0)
    for g in range(FG):
        gs = slice(g * FDG, (g + 1) * FDG)
        t = _dot(u_ref[:, gs].astype(bf16), cs_ref[...])
        uc_s[:, gs] = t[:, :FDG].astype(bf16)
        us_s[:, gs] = t[:, FDG:].astype(bf16)

    @pl.when(j < N_PROMPT_SB)
    def _():
        for s in range(SB // SEQ):
            sl = slice(s * SEQ, (s + 1) * SEQ)
            o_ref[sl, :] = _dot(c256_ref[...], uc_s[sl, :]) - _dot(s256_ref[...], us_s[sl, :])

    @pl.when(j >= N_PROMPT_SB)
    def _():
        o_ref[...] = _dot(c1k_ref[...], uc_s[...]) - _dot(s1k_ref[...], us_s[...])


def _fnet(p, cs128, c256, s256, c1k, s1k):
    full = lambda a: pl.BlockSpec(a.shape, lambda j: (0,) * a.ndim)
    return pl.pallas_call(
        _fnet_kernel,
        grid=(NSB,),
        in_specs=[pl.BlockSpec((SB, F_W), lambda j: (j, 5120 // F_W)),
                  full(cs128), full(c256), full(s256), full(c1k), full(s1k)],
        out_specs=pl.BlockSpec((SB, F_W), lambda j: (j, 0)),
        out_shape=jax.ShapeDtypeStruct((NTOK, F_W), f32),
        scratch_shapes=[pltpu.VMEM((SB, F_W), bf16)] * 2,
        compiler_params=_params(1),
        name="fnet",
    )(p, cs128, c256, s256, c1k, s1k)


def _dft_tables():
    def cs(n, scale):
        k = np.arange(n, dtype=np.int64)
        ang = 2.0 * np.pi * ((k[:, None] * k[None, :]) % n).astype(np.float64) / n
        return np.cos(ang) * scale, np.sin(ang) * scale

    c128, s128 = cs(FDG, 1.0)
    c256, s256 = cs(SEQ, (SEQ * FDG) ** -0.5)
    c1k, s1k = cs(DEC_SEQ, (DEC_SEQ * FDG) ** -0.5)
    to = lambda a: jnp.asarray(a, dtype=f32).astype(bf16)
    return to(np.concatenate([c128, s128], axis=1)), to(c256), to(s256), to(c1k), to(s1k)


def _mix_kernel(o_ref, r_ref, zg_ref, zf_ref, f_ref, x_ref, mod_ref, gn_ref, n2_ref,
                wg_ref, wf_ref, wo_ref, wq_ref, x1_ref, h2_ref, qry_ref):
    r = r_ref[...]
    og = o_ref[...] * gn_ref[...] * (r * jax.nn.sigmoid(r))
    y_gla = _dot(og.astype(bf16), wg_ref[...])
    y_four = _dot(f_ref[...].astype(bf16), wf_ref[...])
    merged = jax.nn.sigmoid(zg_ref[...]) * y_gla + jax.nn.sigmoid(zf_ref[...]) * y_four
    mix = _dot(merged.astype(bf16), wo_ref[...])
    g1 = mod_ref[0, :, 2 * D:3 * D]
    sh2 = mod_ref[0, :, 3 * D:4 * D]
    sc2 = mod_ref[0, :, 4 * D:5 * D]
    x1 = x_ref[...] + g1 * mix
    x1_ref[...] = x1
    y = x1 * lax.rsqrt(jnp.mean(x1 * x1, axis=-1, keepdims=True) + EPS) * n2_ref[...]
    h2 = (y * (1.0 + sc2) + sh2).astype(bf16)
    h2_ref[...] = h2
    qry = _dot(h2, wq_ref[...])
    for h in range(PH):
        qry_ref[h] = qry[:, h * DQ:(h + 1) * DQ]


def _mix(o, p, four, x, mods, gla_norm, norm2, w_gla_out, w_four_out, w_out, w_query):
    tm = 256
    full = lambda a: pl.BlockSpec(a.shape, lambda i: (0,) * a.ndim)
    return pl.pallas_call(
        _mix_kernel,
        grid=(NTOK // tm,),
        in_specs=[pl.BlockSpec((tm, V_W), lambda i: (i, 0)),
                  pl.BlockSpec((tm, V_W), lambda i: (i, 2)),
                  pl.BlockSpec((tm, D), lambda i: (i, 3)),
                  pl.BlockSpec((tm, D), lambda i: (i, 4)),
                  pl.BlockSpec((tm, F_W), lambda i: (i, 0)),
                  pl.BlockSpec((tm, D), lambda i: (i, 0)),
                  pl.BlockSpec((1, 1, 6 * D), lambda i: (_cond_of_block(i, tm), 0, 0)),
                  full(gla_norm), full(norm2), full(w_gla_out), full(w_four_out), full(w_out),
                  full(w_query)],
        out_specs=[pl.BlockSpec((tm, D), lambda i: (i, 0)),
                   pl.BlockSpec((tm, D), lambda i: (i, 0)),
                   pl.BlockSpec((PH, tm, DQ), lambda i: (0, i, 0))],
        out_shape=[jax.ShapeDtypeStruct((NTOK, D), f32),
                   jax.ShapeDtypeStruct((NTOK, D), bf16),
                   jax.ShapeDtypeStruct((PH, NTOK, DQ), f32)],
        compiler_params=_params(1),
        name="mix",
    )(o, p, p, p, four, x, mods, gla_norm, norm2, w_gla_out, w_four_out, w_out, w_query)


def _top16_rows(x):
    rows = []
    for r in range(TOPK):
        m = jnp.max(x, axis=0, keepdims=True)
        rows.append(m)
        if r + 1 < TOPK:
            x = jnp.where(x == m, -jnp.inf, x)
    return rows


def _stack16(rows):
    sub = lax.broadcasted_iota(jnp.int32, (TOPK, rows[0].shape[1]), 0)
    out = jnp.zeros((TOPK, rows[0].shape[1]), f32)
    for r, row in enumerate(rows):
        out = jnp.where(sub == r, row, out)
    return out


def _score_kernel(qry_ref, keys_ref, s1_ref, s2_ref, e2_ref, f1_ref, tau_ref):
    tm = qry_ref.shape[1]
    lanes = 128

    def head(h, carry):
        q = qry_ref[h].astype(bf16)
        s1 = _dot_nt(keys_ref[h, 0], q[:, :DQ // 2])
        s2 = _dot_nt(keys_ref[h, 1], q[:, DQ // 2:])
        s1_ref[h] = s1
        s2_ref[h] = s2
        for lb in range(tm // lanes):
            ls = slice(lb * lanes, (lb + 1) * lanes)
            a = _top16_rows(s1[:, ls])
            b = _top16_rows(s2[:, ls])
            b16 = _stack16(b)
            cands = []
            for r1 in range(TOPK):
                n2 = TOPK // (r1 + 1)
                rows = TOPK if n2 > 8 else 8
                sub = lax.broadcasted_iota(jnp.int32, (rows, lanes), 0)
                cands.append(jnp.where(sub < n2, a[r1] + b16[:rows], -jnp.inf))
            z_sum = jnp.zeros((1, lanes), f32)
            top = None
            for r in range(TOPK):
                m = functools.reduce(jnp.maximum, [jnp.max(c, axis=0, keepdims=True) for c in cands])
                if r == 0:
                    top = m
                z_sum = z_sum + jnp.exp(m - top)
                if r + 1 < TOPK:
                    cands = [jnp.where(c == m, -jnp.inf, c) for c in cands]
            tau_ref[h, :, ls] = m
            e2_ref[h, :, ls] = jnp.exp(s2[:, ls] - b[0])
            f1_ref[h, :, ls] = jnp.exp(s1[:, ls] - a[0]) / z_sum
        return carry

    lax.fori_loop(0, PH, head, 0)


def _scores(qry, keys, ntok):
    tm = 512
    big = pl.BlockSpec((PH, NKEYS, tm), lambda i: (0, 0, i))
    shp = jax.ShapeDtypeStruct((PH, NKEYS, ntok), f32)
    return pl.pallas_call(
        _score_kernel,
        grid=(ntok // tm,),
        in_specs=[pl.BlockSpec((PH, tm, DQ), lambda i: (0, i, 0)),
                  pl.BlockSpec(keys.shape, lambda i: (0, 0, 0, 0))],
        out_specs=[big, big, big, big, pl.BlockSpec((PH, 1, tm), lambda i: (0, 0, i))],
        out_shape=[shp, shp, shp, shp, jax.ShapeDtypeStruct((PH, 1, ntok), f32)],
        compiler_params=_params(1),
        name="peer_scores",
    )(qry, keys)


def _gelu_tanh(x):
    return 0.5 * x * (1.0 + jnp.tanh(0.7978845608028654 * (x + 0.044715 * (x * x * x))))


def _peer_kernel(u_ref, vt_ref, h2_ref, s1_ref, f1_ref, s2_ref, e2_ref, tau_ref, x1_ref, mod_ref,
                 fn_ref, o_ref, acc_s, at_s, coef_s, *, final_norm):
    c = pl.program_id(1)
    tm = h2_ref.shape[0]
    te = u_ref.shape[0]
    lanes = 128

    @pl.when(c == 0)
    def _():
        acc_s[...] = jnp.zeros_like(acc_s)

    at_s[...] = _dot_nt(u_ref[...], h2_ref[...])

    def row_group(g, carry):
        r0 = pl.multiple_of(g * NKEYS, NKEYS)
        to_row0 = (8 - g) % 8
        for lb in range(tm // lanes):
            ls = slice(lb * lanes, (lb + 1) * lanes)
            w = jnp.zeros((NKEYS, lanes), f32)
            for h in range(PH):
                s1_row = pltpu.roll(s1_ref[h, :, ls], to_row0, 0)[0:1]
                f1_row = pltpu.roll(f1_ref[h, :, ls], to_row0, 0)[0:1]
                z = s2_ref[h, :, ls] + s1_row
                val = e2_ref[h, :, ls] * f1_row
                w = w + jnp.where(z >= tau_ref[h, :, ls], val, 0.0)
            a = at_s[pl.ds(r0, NKEYS), ls]
            coef_s[pl.ds(r0, NKEYS), ls] = (_gelu_tanh(a) * w).astype(bf16)
        return carry

    lax.fori_loop(0, te // NKEYS, row_group, 0)
    acc_s[...] += _dot(vt_ref[...], coef_s[...])

    @pl.when(c == pl.num_programs(1) - 1)
    def _():
        g2 = mod_ref[0, :, 5 * D:6 * D]
        x2 = x1_ref[...] + g2 * acc_s[...].T
        if final_norm:
            x2 = x2 * lax.rsqrt(jnp.mean(x2 * x2, axis=-1, keepdims=True) + EPS) * fn_ref[...]
        o_ref[...] = x2


def _peer(u, vt, h2, s1, f1, s2, e2, tau, x1, mods, fn, final_norm, ntok, block0=0):
    tm, te = 512, 1024
    rows = te // NKEYS
    return pl.pallas_call(
        functools.partial(_peer_kernel, final_norm=final_norm),
        grid=(ntok // tm, NEXP // te),
        in_specs=[pl.BlockSpec((te, D), lambda i, c: (c, 0)),
                  pl.BlockSpec((D, te), lambda i, c: (0, c)),
                  pl.BlockSpec((tm, D), lambda i, c: (i, 0)),
                  pl.BlockSpec((PH, rows, tm), lambda i, c: (0, c, i)),
                  pl.BlockSpec((PH, rows, tm), lambda i, c: (0, c, i)),
                  pl.BlockSpec((PH, NKEYS, tm), lambda i, c: (0, 0, i)),
                  pl.BlockSpec((PH, NKEYS, tm), lambda i, c: (0, 0, i)),
                  pl.BlockSpec((PH, 1, tm), lambda i, c: (0, 0, i)),
                  pl.BlockSpec((tm, D), lambda i, c: (i, 0)),
                  pl.BlockSpec((1, 1, 6 * D), lambda i, c: (_cond_of_block(i + block0, tm), 0, 0)),
                  pl.BlockSpec((1, D), lambda i, c: (0, 0))],
        out_specs=pl.BlockSpec((tm, D), lambda i, c: (i, 0)),
        out_shape=jax.ShapeDtypeStruct((ntok, D), f32),
        scratch_shapes=[pltpu.VMEM((D, tm), f32), pltpu.VMEM((te, tm), f32), pltpu.VMEM((te, tm), bf16)],
        compiler_params=_params(2),
        name="peer_dense",
    )(u, vt, h2, s1, f1, s2, e2, tau, x1, mods, fn)


def _rope_tables():
    pos = jnp.arange(DEC_SEQ)
    half = DK // 4
    freqs = ROPE_THETA ** (-jnp.arange(half, dtype=f32) / half)

    def part(p):
        ang = p.astype(f32)[:, None] * freqs[None, :]
        c, s = jnp.cos(ang), jnp.sin(ang)
        return jnp.concatenate([c, c], axis=-1), jnp.concatenate([-s, s], axis=-1)

    cr, sr = part(pos // GRID_W)
    cc, sc = part(pos % GRID_W)
    cos = jnp.concatenate([cr, cc], axis=-1)
    sin = jnp.concatenate([sr, sc], axis=-1)
    return (jnp.stack([jnp.ones_like(cos), cos]), jnp.stack([jnp.zeros_like(sin), sin]))


def _permute_w_in(w):
    lo = 2 * QK_W + 2 * V_W
    lr_hi = lo + 2 * LOWRANK
    u_hi = lr_hi + F_W
    return jnp.concatenate(
        [w[:, :lo], w[:, u_hi:], w[:, lr_hi:u_hi], w[:, lo:lr_hi],
         jnp.zeros((D, PW - w.shape[1]), w.dtype)], axis=1).astype(bf16)


def _pad_lowrank(w, row0):
    return jnp.zeros((128, QK_W), f32).at[row0:row0 + LOWRANK].set(w).astype(bf16)


def _layer_mixer(x, mods_l, l, tables, state_gla, norm1, w_in, w_af, b_af, w_ab, b_ab, gla_norm,
                 w_gla_out, w_four_out, w_out, norm2, w_query):
    cos_t, sin_t, dft = tables
    p = _inproj(x, mods_l, norm1[l][None], _permute_w_in(w_in[l]))
    o, states = _gla(p, _pad_lowrank(w_af[l], 0), _pad_lowrank(w_ab[l], LOWRANK), b_af[l][None],
                     b_ab[l][None], cos_t, sin_t, state_gla[:, l])
    four = _fnet(p, *dft)
    x1, h2, qry = _mix(o, p, four, x, mods_l, gla_norm[l][None], norm2[l][None],
                       w_gla_out[l].astype(bf16), w_four_out[l].astype(bf16), w_out[l].astype(bf16),
                       w_query[l].astype(bf16))
    return x1, h2, qry, states


def _layer_peer(x1, h2, qry, mods_l, l, sub_keys, expert_u, expert_v, final_norm):
    s1, s2, e2, f1, tau = _scores(qry, sub_keys[l].astype(bf16), NTOK)
    return _peer(expert_u[l].astype(bf16), expert_v[l].astype(bf16).T, h2, s1, f1, s2, e2, tau, x1,
                 mods_l, final_norm[None], l == DEPTH - 1, NTOK)


def kernel(x_prompt, x_sample, state_gla, c, c_ctx, w_mod, b_mod, norm1, w_in, w_af, b_af, w_ab, b_ab, gla_norm, w_gla_out, w_four_out, w_out, norm2, w_query, sub_keys, expert_u, expert_v, final_norm):
    x = jnp.concatenate([x_prompt.reshape(N_PROMPT, D), x_sample.reshape(-1, D)], axis=0)
    cond8 = jnp.concatenate([c_ctx[None], c, jnp.zeros((8 - 1 - DEC_BATCH, D), f32)], axis=0)
    mods = _modulation(cond8, w_mod, b_mod)
    tables = _rope_tables() + (_dft_tables(),)
    states = []
    for l in range(DEPTH):
        mods_l = mods[l].reshape(8, 1, 6 * D)
        x1, h2, qry, st = _layer_mixer(x, mods_l, l, tables, state_gla, norm1, w_in, w_af, b_af, w_ab,
                                       b_ab, gla_norm, w_gla_out, w_four_out, w_out, norm2, w_query)
        states.append(st[:N_PROMPT_SB].reshape(BATCH, 2, H, DK, DV))
        x = _layer_peer(x1, h2, qry, mods_l, l, sub_keys, expert_u, expert_v, final_norm)
    y_prompt = x[:N_PROMPT].reshape(BATCH, SEQ, D)
    y_sample = x[N_PROMPT:].reshape(DEC_BATCH, DEC_SEQ, D)
    return (y_prompt, y_sample, jnp.stack(states, axis=1))
```
